```python
import jax, jax.numpy as jnp
from jax import lax
import numpy as np

D_MODEL = 1024
BATCH = 8
SEQ = 2048
DEPTH = 2

EPS = 1e-6
N_EVEN = (DEPTH + 1) // 2
N_ODD = DEPTH // 2

A_HEADS = 8
A_HEAD_DIM = 64
A_WIDTH = A_HEADS * A_HEAD_DIM
CONV_WIDTH = 3
POOL_WINDOWS = (2, 4, 8, 16)
B_GROUPS = len(POOL_WINDOWS)
B_GROUP_DIM = 128
B_WIDTH = B_GROUPS * B_GROUP_DIM
AB_IN = 3 * A_WIDTH + B_WIDTH
AB_OUT = A_WIDTH + B_WIDTH

C_HEADS = 8
C_NOPE = 64
C_ROPE = 32
C_V = 64
C_Q_RANK = 256
C_KV_RANK = 128
C_WIDTH = C_HEADS * C_V
ROPE_THETA = 10000.0
ATTN_BLOCK = 128
D_GROUPS = 4
D_GROUP_DIM = 128
D_WIDTH = D_GROUPS * D_GROUP_DIM
D_CHUNK = 128
CD_IN = C_Q_RANK + C_KV_RANK + C_ROPE + 2 * D_WIDTH
CD_OUT = C_WIDTH + D_WIDTH

D_FF = 2816
N_MOD = 6

kernel_name = "hybrid_conv_pool_mla_gmlp_block"


def rms_norm(x, g):
    xf = x.astype(jnp.float32)
    y = xf * lax.rsqrt(jnp.mean(xf * xf, axis=-1, keepdims=True) + EPS)
    return (y * g.astype(jnp.float32)).astype(x.dtype)


def layer_norm(x, g, b):
    xf = x.astype(jnp.float32)
    mu = jnp.mean(xf, axis=-1, keepdims=True)
    xc = xf - mu
    y = xc * lax.rsqrt(jnp.mean(xc * xc, axis=-1, keepdims=True) + EPS)
    return (y * g.astype(jnp.float32) + b.astype(jnp.float32)).astype(x.dtype)


def causal_dwconv(x, w):
    K, C = w.shape
    return lax.conv_general_dilated(
        x, w[:, None, :].astype(x.dtype), window_strides=(1,),
        padding=((K - 1, 0),), dimension_numbers=("NWC", "WIO", "NWC"),
        feature_group_count=C)


def rope_tables(positions):
    half = C_ROPE // 2
    inv_freq = ROPE_THETA ** (-jnp.arange(half, dtype=jnp.float32) / half)
    ang = positions.astype(jnp.float32)[..., None] * inv_freq
    return jnp.cos(ang), jnp.sin(ang)


def apply_rope(x, cos, sin):
    if x.ndim == 4:
        cos, sin = cos[:, :, None, :], sin[:, :, None, :]
    xf = x.astype(jnp.float32)
    x1, x2 = jnp.split(xf, 2, axis=-1)
    return jnp.concatenate([x1 * cos - x2 * sin, x2 * cos + x1 * sin], axis=-1).astype(x.dtype)


def short_gated_conv(b_gate, c_gate, h, conv_w):
    return b_gate * causal_dwconv(c_gate * h, conv_w)


def multiscale_pool(p, mix_w, scale):
    Bn, S, _ = p.shape
    pg = p.reshape(Bn, S, B_GROUPS, B_GROUP_DIM)
    cs = jnp.cumsum(pg.astype(jnp.float32), axis=1)
    t = jnp.arange(S)
    pooled = []
    for g, w in enumerate(POOL_WINDOWS):
        csg = cs[:, :, g]
        lag = jnp.pad(csg[:, :S - w], ((0, 0), (w, 0), (0, 0)))
        cnt = jnp.minimum(t + 1, w).astype(jnp.float32)[None, :, None]
        pooled.append((csg - lag) / cnt)
    pooled = jnp.stack(pooled, axis=2).astype(p.dtype) - pg
    y = jnp.einsum("bsgc,gcd->bsgd", pooled, mix_w)
    return y.reshape(Bn, S, B_WIDTH) * scale


def latent_attention(q_lat, kv_lat, k_pe, q_norm_g, w_uq, kv_norm_g, w_ukv, cos, sin):
    Bn, S, _ = q_lat.shape
    q = (rms_norm(q_lat, q_norm_g) @ w_uq).reshape(Bn, S, C_HEADS, C_NOPE + C_ROPE)
    q_nope, q_pe = q[..., :C_NOPE], apply_rope(q[..., C_NOPE:], cos, sin)
    kv = (rms_norm(kv_lat, kv_norm_g) @ w_ukv).reshape(Bn, S, C_HEADS, C_NOPE + C_V)
    k_nope, v = kv[..., :C_NOPE], kv[..., C_NOPE:]
    k_pe = apply_rope(k_pe, cos, sin)
    scale = (C_NOPE + C_ROPE) ** -0.5
    nb = S // ATTN_BLOCK
    qn_blocks = q_nope.reshape(Bn, nb, ATTN_BLOCK, C_HEADS, C_NOPE).transpose(1, 0, 2, 3, 4)
    qp_blocks = q_pe.reshape(Bn, nb, ATTN_BLOCK, C_HEADS, C_ROPE).transpose(1, 0, 2, 3, 4)
    key_pos = jnp.arange(S)

    def one_block(args):
        i, qn, qp = args
        s = (jnp.einsum("bqhd,bkhd->bhqk", qn, k_nope, preferred_element_type=jnp.float32)
             + jnp.einsum("bqhr,bkr->bhqk", qp, k_pe, preferred_element_type=jnp.float32)) * scale
        q_pos = i * ATTN_BLOCK + jnp.arange(ATTN_BLOCK)
        s = jnp.where(key_pos[None, :] <= q_pos[:, None], s, -jnp.inf)
        prob = jax.nn.softmax(s, axis=-1).astype(v.dtype)
        return jnp.einsum("bhqk,bkhd->bqhd", prob, v)

    out = lax.map(one_block, (jnp.arange(nb), qn_blocks, qp_blocks))
    return out.transpose(1, 0, 2, 3, 4).reshape(Bn, S, C_WIDTH)


def spatial_gating(u, v, ln_g, ln_b, w_s, b_s):
    Bn, S, _ = v.shape
    v = layer_norm(v, ln_g, ln_b)
    vc = v.reshape(Bn, S // D_CHUNK, D_CHUNK, D_GROUPS, D_GROUP_DIM)
    mask = jnp.tril(jnp.ones((D_CHUNK, D_CHUNK), dtype=bool))
    w = jnp.where(mask[None], w_s, 0)
    mixed = jnp.einsum("gts,bnsgc->bntgc", w, vc) + b_s.T[None, None, :, :, None]
    return u * mixed.reshape(Bn, S, D_WIDTH)


def conv_ffn(h, w_up, conv_w, w_down):
    z = causal_dwconv(h @ w_up, conv_w)
    g, u = jnp.split(z, 2, axis=-1)
    return (jax.nn.silu(g) * u) @ w_down


def setup_inputs(seed: int = 0) -> dict:
    key = jax.random.key(seed)
    ks = iter(jax.random.split(key, 32))
    nrm = lambda shape, s: jax.random.normal(next(ks), shape, jnp.float32) * s
    gain = lambda shape: 1.0 + nrm(shape, 0.05)
    D = D_MODEL
    offsets = jax.random.randint(next(ks), (BATCH, 1), 0, 4096, dtype=jnp.int32)
    positions = (jnp.arange(SEQ, dtype=jnp.int32)[None, :] + offsets).astype(jnp.int32)
    return {
        "x": nrm((BATCH, SEQ, D), 1.0),
        "c": nrm((BATCH, D), 1.0),
        "positions": positions,
        "ada_w": nrm((DEPTH, D, N_MOD * D), 0.5 * D ** -0.5),
        "ada_b": nrm((DEPTH, N_MOD * D), 0.02),
        "norm1_g": gain((DEPTH, D)),
        "norm2_g": gain((DEPTH, D)),
        "ab_w_in": nrm((N_EVEN, D, AB_IN), D ** -0.5),
        "a_conv_w": nrm((N_EVEN, CONV_WIDTH, A_WIDTH), CONV_WIDTH ** -0.5),
        "b_mix_w": nrm((N_EVEN, B_GROUPS, B_GROUP_DIM, B_GROUP_DIM), B_GROUP_DIM ** -0.5),
        "b_scale": 1.0 + nrm((N_EVEN, B_WIDTH), 0.1),
        "ab_w_out": nrm((N_EVEN, AB_OUT, D), AB_OUT ** -0.5),
        "cd_w_in": nrm((N_ODD, D, CD_IN), D ** -0.5),
        "c_q_norm_g": gain((N_ODD, C_Q_RANK)),
        "c_w_uq": nrm((N_ODD, C_Q_RANK, C_HEADS * (C_NOPE + C_ROPE)), C_Q_RANK ** -0.5),
        "c_kv_norm_g": gain((N_ODD, C_KV_RANK)),
        "c_w_ukv": nrm((N_ODD, C_KV_RANK, C_HEADS * (C_NOPE + C_V)), C_KV_RANK ** -0.5),
        "d_ln_g": gain((N_ODD, D_WIDTH)),
        "d_ln_b": nrm((N_ODD, D_WIDTH), 0.02),
        "d_w_s": nrm((N_ODD, D_GROUPS, D_CHUNK, D_CHUNK), 0.5 * D_CHUNK ** -0.5),
        "d_b_s": 1.0 + nrm((N_ODD, D_GROUPS, D_CHUNK), 0.02),
        "cd_w_out": nrm((N_ODD, CD_OUT, D), CD_OUT ** -0.5),
        "ffn_w_up": nrm((DEPTH, D, 2 * D_FF), D ** -0.5),
        "ffn_conv_w": nrm((DEPTH, CONV_WIDTH, 2 * D_FF), CONV_WIDTH ** -0.5),
        "ffn_w_down": nrm((DEPTH, D_FF, D), D_FF ** -0.5),
        "final_norm_g": gain((D,)),
    }


def reference(x, c, positions, ada_w, ada_b, norm1_g, norm2_g,
              ab_w_in, a_conv_w, b_mix_w, b_scale, ab_w_out,
              cd_w_in, c_q_norm_g, c_w_uq, c_kv_norm_g, c_w_ukv,
              d_ln_g, d_ln_b, d_w_s, d_b_s, cd_w_out,
              ffn_w_up, ffn_conv_w, ffn_w_down, final_norm_g):
    cos, sin = rope_tables(positions)
    c_act = jax.nn.silu(c)
    for l in range(DEPTH):
        mod = c_act @ ada_w[l] + ada_b[l]
        sh1, sc1, g1, sh2, sc2, g2 = [m[:, None, :] for m in jnp.split(mod, N_MOD, axis=-1)]
        h = rms_norm(x, norm1_g[l]) * (1 + sc1) + sh1
        i = l // 2
        if l % 2 == 0:
            z = h @ ab_w_in[i]
            b_gate, c_gate, a_in, p = jnp.split(z, [A_WIDTH, 2 * A_WIDTH, 3 * A_WIDTH], axis=-1)
            y_a = short_gated_conv(b_gate, c_gate, a_in, a_conv_w[i])
            y_b = multiscale_pool(p, b_mix_w[i], b_scale[i])
            y = jnp.concatenate([y_a, y_b], axis=-1) @ ab_w_out[i]
        else:
            z = h @ cd_w_in[i]
            q_lat, kv_lat, k_pe, uv = jnp.split(
                z, [C_Q_RANK, C_Q_RANK + C_KV_RANK, C_Q_RANK + C_KV_RANK + C_ROPE], axis=-1)
            y_c = latent_attention(q_lat, kv_lat, k_pe, c_q_norm_g[i], c_w_uq[i],
                                   c_kv_norm_g[i], c_w_ukv[i], cos, sin)
            u, v = jnp.split(jax.nn.gelu(uv), 2, axis=-1)
            y_d = spatial_gating(u, v, d_ln_g[i], d_ln_b[i], d_w_s[i], d_b_s[i])
            y = jnp.concatenate([y_c, y_d], axis=-1) @ cd_w_out[i]
        x = x + g1 * y
        h = rms_norm(x, norm2_g[l]) * (1 + sc2) + sh2
        x = x + g2 * conv_ffn(h, ffn_w_up[l], ffn_conv_w[l], ffn_w_down[l])
    return rms_norm(x, final_norm_g)
```

```python
import functools
import math

import jax
import jax.numpy as jnp
from jax import lax
from jax.experimental import pallas as pl
from jax.experimental.pallas import tpu as pltpu

F32 = jnp.float32
BF16 = jnp.bfloat16

D_MODEL = 1024
BATCH = 8
SEQ = 2048
EPS = 1e-6

A_WIDTH = 512
CONV_WIDTH = 3
POOL_WINDOWS = (2, 4, 8, 16)
B_GROUP_DIM = 128
B_WIDTH = 512

C_HEADS = 8
C_NOPE = 64
C_ROPE = 32
C_V = 64
C_Q_RANK = 256
C_KV_RANK = 128
C_WIDTH = 512
ROPE_THETA = 10000.0
D_GROUPS = 4
D_GROUP_DIM = 128
D_WIDTH = 512
D_CHUNK = 128

D_FF = 2816
N_MOD = 6

LANES = 128
SUBLANES = 8
HEAD_SLOT = LANES
SEQ_TILE = 512
FF_CHUNK = 256
N_FF_CHUNKS = D_FF // FF_CHUNK
POOL_HALO = 16
CONV_HALO = SUBLANES
VMEM_LIMIT = 56 * 1024 * 1024
LOG2E = math.log2(math.e)


def _dot(a, b):
    return jnp.dot(a, b, preferred_element_type=F32)


def _dot_nt(a, b):
    return lax.dot_general(a, b, (((1,), (1,)), ((), ())), preferred_element_type=F32)


def _rms_rows(x):
    return x * lax.rsqrt(jnp.mean(x * x, axis=-1, keepdims=True) + EPS)


def _silu(x):
    return x / (1.0 + jnp.exp(-x))


def _params(n_grid_axes):
    return pltpu.CompilerParams(
        dimension_semantics=("arbitrary",) * n_grid_axes, vmem_limit_bytes=VMEM_LIMIT)


def _const_spec(shape):
    zeros = (0,) * len(shape)
    return pl.BlockSpec(shape, lambda *_: zeros, pipeline_mode=pl.Buffered(1))


def _mod_kernel(c_ref, w_ref, b_ref, o_ref):
    ca = _silu(c_ref[...]).astype(BF16)
    o_ref[0] = _dot(ca, w_ref[0].astype(BF16)) + b_ref[0]


def _modulation(c, ada_w, ada_b):
    depth, d, n = ada_w.shape
    tn = 1536
    return pl.pallas_call(
        _mod_kernel,
        out_shape=jax.ShapeDtypeStruct((depth, BATCH, n), F32),
        grid=(depth, n // tn),
        in_specs=[
            pl.BlockSpec((BATCH, d), lambda l, j: (0, 0)),
            pl.BlockSpec((1, d, tn), lambda l, j: (l, 0, j)),
            pl.BlockSpec((1, 1, tn), lambda l, j: (l, 0, j)),
        ],
        out_specs=pl.BlockSpec((1, BATCH, tn), lambda l, j: (l, 0, j)),
        compiler_params=_params(2),
        name="ada_modulation",
    )(c, ada_w, ada_b.reshape(depth, 1, n))


def _rope_kernel(ang_ref, cos_ref, sin_ref):
    a = ang_ref[...]
    cos_ref[...] = jnp.cos(a)
    sin_ref[...] = jnp.sin(a)


def _rope_tables(positions):
    half = C_ROPE // 2
    inv_freq = ROPE_THETA ** (-jnp.arange(half, dtype=F32) / half)
    ang = positions.astype(F32)[..., None] * inv_freq
    flat = ang.reshape(-1, LANES)
    cos, sin = pl.pallas_call(
        _rope_kernel,
        out_shape=(jax.ShapeDtypeStruct(flat.shape, F32),) * 2,
        name="rope_tables",
    )(flat)
    cos = cos.reshape(BATCH, SEQ, half)
    sin = sin.reshape(BATCH, SEQ, half)
    ones = jnp.ones((BATCH, SEQ, C_NOPE), F32)
    zeros_n = jnp.zeros((BATCH, SEQ, C_NOPE), F32)
    zeros_h = jnp.zeros((BATCH, SEQ, half), F32)
    pad = jnp.zeros((BATCH, SEQ, HEAD_SLOT - C_NOPE - C_ROPE), F32)
    mul = jnp.concatenate([ones, cos, cos, pad], axis=-1)
    lo = jnp.concatenate([zeros_n, -sin, zeros_h, pad], axis=-1)
    hi = jnp.concatenate([zeros_n, zeros_h, sin, pad], axis=-1)
    return jnp.concatenate([mul, lo, hi], axis=-1)


def _apply_rope(x, mul, lo, hi):
    return (x * mul + pltpu.roll(x, HEAD_SLOT - C_ROPE // 2, 1) * lo
            + pltpu.roll(x, C_ROPE // 2, 1) * hi)


def _l0_mixer_kernel(x_ref, mod_ref, ng_ref, win_ref, cw_ref, mixw_ref, bscale_ref, wout_ref,
                     o_ref, cbuf, pbuf):
    s = pl.program_id(1)
    ts = SEQ_TILE

    @pl.when(s == 0)
    def _():
        cbuf[0:CONV_HALO, :] = jnp.zeros((CONV_HALO, A_WIDTH), F32)
        pbuf[0:POOL_HALO, :] = jnp.zeros((POOL_HALO, B_WIDTH), F32)

    x = x_ref[0]
    m = mod_ref[0]
    h = (_rms_rows(x) * (ng_ref[...] * (1.0 + m[1:2])) + m[0:1]).astype(BF16)

    zb = _dot(h, win_ref[:, 0:A_WIDTH])
    zc = _dot(h, win_ref[:, A_WIDTH:2 * A_WIDTH])
    za = _dot(h, win_ref[:, 2 * A_WIDTH:3 * A_WIDTH])
    zp = _dot(h, win_ref[:, 3 * A_WIDTH:])

    ch = zc * za
    cbuf[CONV_HALO:CONV_HALO + ts, :] = ch
    c1 = cbuf[CONV_HALO - 1:CONV_HALO - 1 + ts, :]
    c2 = cbuf[CONV_HALO - 2:CONV_HALO - 2 + ts, :]
    cw = cw_ref[...]
    ya = zb * (cw[0:1] * c2 + cw[1:2] * c1 + cw[2:3] * ch)
    cbuf[0:CONV_HALO, :] = ch[ts - CONV_HALO:, :]

    pbuf[POOL_HALO:POOL_HALO + ts, :] = zp
    t = s * ts + lax.broadcasted_iota(jnp.int32, (ts, 1), 0)
    yb = []
    for g, w in enumerate(POOL_WINDOWS):
        lanes = slice(g * B_GROUP_DIM, (g + 1) * B_GROUP_DIM)
        pg = zp[:, lanes]
        acc = pg
        for k in range(1, w):
            acc = acc + pbuf[POOL_HALO - k:POOL_HALO - k + ts, lanes]
        cnt = jnp.minimum(t + 1, w).astype(F32)
        pm = (acc / cnt - pg).astype(BF16)
        yb.append(_dot(pm, mixw_ref[g]))
    yb = jnp.concatenate(yb, axis=-1) * bscale_ref[...]
    pbuf[0:POOL_HALO, :] = zp[ts - POOL_HALO:, :]

    cat = jnp.concatenate([ya.astype(BF16), yb.astype(BF16)], axis=-1)
    o_ref[0] = x + m[2:3] * _dot(cat, wout_ref[...])


def _l0_mixer(x, mod, norm_g, w_in, conv_w, mix_w, b_scale, w_out):
    ts = SEQ_TILE
    tile = pl.BlockSpec((1, ts, D_MODEL), lambda b, s: (b, s, 0))
    return pl.pallas_call(
        _l0_mixer_kernel,
        out_shape=jax.ShapeDtypeStruct(x.shape, F32),
        grid=(BATCH, SEQ // ts),
        in_specs=[
            tile,
            pl.BlockSpec((1, N_MOD, D_MODEL), lambda b, s: (b, 0, 0)),
            _const_spec((1, D_MODEL)),
            _const_spec(w_in.shape),
            _const_spec(conv_w.shape),
            _const_spec(mix_w.shape),
            _const_spec((1, B_WIDTH)),
            _const_spec(w_out.shape),
        ],
        out_specs=tile,
        scratch_shapes=[
            pltpu.VMEM((CONV_HALO + ts, A_WIDTH), F32),
            pltpu.VMEM((POOL_HALO + ts, B_WIDTH), F32),
        ],
        compiler_params=_params(2),
        name="l0_mixer",
    )(x, mod, norm_g.reshape(1, -1), w_in, conv_w, mix_w, b_scale.reshape(1, -1), w_out)


def _ffn_kernel(*refs, final_norm):
    if final_norm:
        x_ref, mod_ref, ng_ref, wup_ref, cw_ref, wdn_ref, fg_ref, o_ref, zbuf, zprev, acc_ref = refs
    else:
        x_ref, mod_ref, ng_ref, wup_ref, cw_ref, wdn_ref, o_ref, zbuf, zprev, acc_ref = refs
    s = pl.program_id(1)
    ts = SEQ_TILE

    @pl.when(s == 0)
    def _():
        zprev[...] = jnp.zeros(zprev.shape, F32)

    x = x_ref[0]
    m = mod_ref[0]
    h = (_rms_rows(x) * (ng_ref[...] * (1.0 + m[4:5])) + m[3:4]).astype(BF16)

    for c in range(N_FF_CHUNKS):
        gu = []
        for j in range(2):
            z = _dot(h, wup_ref[j, c])
            zbuf[j, 0:CONV_HALO, :] = zprev[j, c]
            zbuf[j, CONV_HALO:CONV_HALO + ts, :] = z
            z1 = zbuf[j, CONV_HALO - 1:CONV_HALO - 1 + ts, :]
            z2 = zbuf[j, CONV_HALO - 2:CONV_HALO - 2 + ts, :]
            zprev[j, c] = z[ts - CONV_HALO:, :]
            cw = cw_ref[j, c]
            gu.append(cw[0:1] * z2 + cw[1:2] * z1 + cw[2:3] * z)
        a = (_silu(gu[0]) * gu[1]).astype(BF16)
        contrib = _dot(a, wdn_ref[c])
        if c == 0:
            acc_ref[...] = contrib
        else:
            acc_ref[...] += contrib

    y = x + m[5:6] * acc_ref[...]
    if final_norm:
        y = _rms_rows(y) * fg_ref[...]
    o_ref[0] = y


def _conv_ffn(x, mod, norm_g, w_up, conv_w, w_down, final_g=None):
    ts = SEQ_TILE
    final_norm = final_g is not None
    tile = pl.BlockSpec((1, ts, D_MODEL), lambda b, s: (b, s, 0))
    in_specs = [
        tile,
        pl.BlockSpec((1, N_MOD, D_MODEL), lambda b, s: (b, 0, 0)),
        _const_spec((1, D_MODEL)),
        _const_spec(w_up.shape),
        _const_spec(conv_w.shape),
        _const_spec(w_down.shape),
    ]
    args = [x, mod, norm_g.reshape(1, -1), w_up, conv_w, w_down]
    if final_norm:
        in_specs.append(_const_spec((1, D_MODEL)))
        args.append(final_g.reshape(1, -1))
    return pl.pallas_call(
        functools.partial(_ffn_kernel, final_norm=final_norm),
        out_shape=jax.ShapeDtypeStruct(x.shape, F32),
        grid=(BATCH, SEQ // ts),
        in_specs=in_specs,
        out_specs=tile,
        scratch_shapes=[
            pltpu.VMEM((2, CONV_HALO + ts, FF_CHUNK), F32),
            pltpu.VMEM((2, N_FF_CHUNKS, CONV_HALO, FF_CHUNK), F32),
            pltpu.VMEM((ts, D_MODEL), F32),
        ],
        compiler_params=_params(2),
        name="conv_ffn_final" if final_norm else "conv_ffn",
    )(*args)


def _l1_in_kernel(x_ref, mod_ref, ng_ref, win_ref, qg_ref, wuq_ref, kvg_ref, wk_ref, wv_ref,
                  rope_ref, lng_ref, lnb_ref, ws_ref, bias_ref,
                  q_ref, k_ref, v_ref, yd_ref):
    ts = SEQ_TILE
    x = x_ref[0]
    m = mod_ref[0]
    h = (_rms_rows(x) * (ng_ref[...] * (1.0 + m[1:2])) + m[0:1]).astype(BF16)

    rope = rope_ref[0]
    mul, lo, hi = rope[:, 0:HEAD_SLOT], rope[:, HEAD_SLOT:2 * HEAD_SLOT], rope[:, 2 * HEAD_SLOT:]

    q_lat = _dot(h, win_ref[:, 0:C_Q_RANK])
    qn = (_rms_rows(q_lat) * qg_ref[...]).astype(BF16)
    q = _dot(qn, wuq_ref[...])
    for hd in range(C_HEADS):
        lanes = slice(hd * HEAD_SLOT, (hd + 1) * HEAD_SLOT)
        q_ref[0, :, lanes] = _apply_rope(q[:, lanes], mul, lo, hi).astype(BF16)

    c0 = C_Q_RANK
    kv_lat = _dot(h, win_ref[:, c0:c0 + C_KV_RANK])
    kvn = (_rms_rows(kv_lat) * kvg_ref[...]).astype(BF16)
    c0 += C_KV_RANK
    kpe = _apply_rope(_dot(h, win_ref[:, c0:c0 + HEAD_SLOT]), mul, lo, hi)
    kn = _dot(kvn, wk_ref[...])
    for hd in range(C_HEADS):
        lanes = slice(hd * HEAD_SLOT, (hd + 1) * HEAD_SLOT)
        k_ref[0, :, lanes] = (kn[:, lanes] + kpe).astype(BF16)
    v_ref[0] = _dot(kvn, wv_ref[...]).astype(BF16)
    c0 += HEAD_SLOT

    uv = _dot(h, win_ref[:, c0:])
    uv = 0.5 * uv * (1.0 + jnp.tanh(math.sqrt(2.0 / math.pi) * (uv + 0.044715 * (uv * uv * uv))))
    u, v2 = uv[:, 0:D_WIDTH], uv[:, D_WIDTH:]
    mu = jnp.mean(v2, axis=-1, keepdims=True)
    vc = v2 - mu
    vn = vc * lax.rsqrt(jnp.mean(vc * vc, axis=-1, keepdims=True) + EPS)
    vn = (vn * lng_ref[...] + lnb_ref[...]).astype(BF16)
    row = lax.broadcasted_iota(jnp.int32, (D_CHUNK, D_CHUNK), 0)
    col = lax.broadcasted_iota(jnp.int32, (D_CHUNK, D_CHUNK), 1)
    bias = bias_ref[...]
    for g in range(D_GROUPS):
        lanes = slice(g * D_GROUP_DIM, (g + 1) * D_GROUP_DIM)
        wg = jnp.where(col <= row, ws_ref[g], 0.0).astype(BF16)
        for ck in range(ts // D_CHUNK):
            rows = slice(ck * D_CHUNK, (ck + 1) * D_CHUNK)
            mixed = _dot(wg, vn[rows, lanes]) + bias[:, lanes]
            yd_ref[0, rows, lanes] = (u[rows, lanes] * mixed).astype(BF16)


def _l1_in(x, mod, norm_g, w_in, q_g, w_uq, kv_g, w_k, w_v, rope, ln_g, ln_b, w_s, bias_map):
    ts = SEQ_TILE
    grid_tile = lambda width: pl.BlockSpec((1, ts, width), lambda b, s: (b, s, 0))
    qk_width = C_HEADS * HEAD_SLOT
    return pl.pallas_call(
        _l1_in_kernel,
        out_shape=(
            jax.ShapeDtypeStruct((BATCH, SEQ, qk_width), BF16),
            jax.ShapeDtypeStruct((BATCH, SEQ, qk_width), BF16),
            jax.ShapeDtypeStruct((BATCH, SEQ, C_WIDTH), BF16),
            jax.ShapeDtypeStruct((BATCH, SEQ, D_WIDTH), BF16),
        ),
        grid=(BATCH, SEQ // ts),
        in_specs=[
            grid_tile(D_MODEL),
            pl.BlockSpec((1, N_MOD, D_MODEL), lambda b, s: (b, 0, 0)),
            _const_spec((1, D_MODEL)),
            _const_spec(w_in.shape),
            _const_spec((1, C_Q_RANK)),
            _const_spec(w_uq.shape),
            _const_spec((1, C_KV_RANK)),
            _const_spec(w_k.shape),
            _const_spec(w_v.shape),
            grid_tile(3 * HEAD_SLOT),
            _const_spec((1, D_WIDTH)),
            _const_spec((1, D_WIDTH)),
            _const_spec(w_s.shape),
            _const_spec(bias_map.shape),
        ],
        out_specs=(grid_tile(qk_width), grid_tile(qk_width), grid_tile(C_WIDTH), grid_tile(D_WIDTH)),
        compiler_params=_params(2),
        name="l1_in_proj",
    )(x, mod, norm_g.reshape(1, -1), w_in, q_g.reshape(1, -1), w_uq, kv_g.reshape(1, -1), w_k, w_v,
      rope, ln_g.reshape(1, -1), ln_b.reshape(1, -1), w_s, bias_map)


def _attn_kernel(q_ref, k_ref, v_ref, yd_ref, x_ref, mod_ref, wout_ref, o_ref, yc_ref):
    i = pl.program_id(1)
    tq = SEQ_TILE
    tk = SEQ_TILE
    scale2 = (C_NOPE + C_ROPE) ** -0.5 * LOG2E
    lane = lax.broadcasted_iota(jnp.int32, (tq, LANES), 1)
    qpos = lax.broadcasted_iota(jnp.int32, (tq, tk), 0)
    kpos = lax.broadcasted_iota(jnp.int32, (tq, tk), 1)

    def block(q, hd, j, carry, masked):
        m_prev, l_prev, acc = carry
        rows = pl.ds(pl.multiple_of(j * tk, tk), tk)
        k = k_ref[0, rows, hd * HEAD_SLOT:(hd + 1) * HEAD_SLOT]
        pair = hd // 2
        v = v_ref[0, rows, pair * LANES:(pair + 1) * LANES]
        sc = _dot_nt(q, k)
        if masked:
            sc = jnp.where(kpos <= qpos, sc, -jnp.inf)
        m_new = jnp.maximum(m_prev, jnp.max(sc, axis=-1, keepdims=True))
        alpha = jnp.exp2((m_prev - m_new) * scale2)
        p = jnp.exp2((sc - m_new) * scale2)
        l_new = alpha * l_prev + jnp.sum(p, axis=-1, keepdims=True)
        acc = alpha * acc + _dot(p.astype(BF16), v)
        return m_new, l_new, acc

    outs = []
    for hd in range(C_HEADS):
        q = q_ref[0, :, hd * HEAD_SLOT:(hd + 1) * HEAD_SLOT]
        init = (jnp.full((tq, 1), -jnp.inf, F32), jnp.zeros((tq, 1), F32), jnp.zeros((tq, LANES), F32))
        carry = lax.fori_loop(0, i, lambda j, c: block(q, hd, j, c, False), init)
        _, l_fin, acc = block(q, hd, i, carry, True)
        outs.append(acc / l_fin)
    for pair in range(C_HEADS // 2):
        yc_ref[:, pair * LANES:(pair + 1) * LANES] = jnp.where(
            lane < C_V, outs[2 * pair], outs[2 * pair + 1]).astype(BF16)

    cat = jnp.concatenate([yc_ref[...], yd_ref[0]], axis=-1)
    m = mod_ref[0]
    o_ref[0] = x_ref[0] + m[2:3] * _dot(cat, wout_ref[...])


def _attention(q, k, v, yd, x, mod, w_out):
    tq = SEQ_TILE
    qk_width = C_HEADS * HEAD_SLOT
    q_tile = lambda width: pl.BlockSpec((1, tq, width), lambda b, s: (b, s, 0))
    return pl.pallas_call(
        _attn_kernel,
        out_shape=jax.ShapeDtypeStruct(x.shape, F32),
        grid=(BATCH, SEQ // tq),
        in_specs=[
            q_tile(qk_width),
            pl.BlockSpec((1, SEQ, qk_width), lambda b, s: (b, 0, 0)),
            pl.BlockSpec((1, SEQ, C_WIDTH), lambda b, s: (b, 0, 0)),
            q_tile(D_WIDTH),
            q_tile(D_MODEL),
            pl.BlockSpec((1, N_MOD, D_MODEL), lambda b, s: (b, 0, 0)),
            _const_spec(w_out.shape),
        ],
        out_specs=q_tile(D_MODEL),
        scratch_shapes=[pltpu.VMEM((tq, C_WIDTH), BF16)],
        compiler_params=_params(2),
        name="l1_attention",
    )(q, k, v, yd, x, mod, w_out)


def _ffn_weights(w_up, conv_w, w_down):
    up = w_up.astype(BF16).reshape(D_MODEL, 2, N_FF_CHUNKS, FF_CHUNK).transpose(1, 2, 0, 3)
    cw = conv_w.reshape(CONV_WIDTH, 2, N_FF_CHUNKS, FF_CHUNK).transpose(1, 2, 0, 3)
    dn = w_down.astype(BF16).reshape(N_FF_CHUNKS, FF_CHUNK, D_MODEL)
    return up, cw, dn


def _pad_heads(w, used):
    kdim = w.shape[0]
    w = w.reshape(kdim, C_HEADS, used)
    w = jnp.pad(w, ((0, 0), (0, 0), (0, HEAD_SLOT - used)))
    return w.reshape(kdim, C_HEADS * HEAD_SLOT)


def kernel(x, c, positions, ada_w, ada_b, norm1_g, norm2_g, ab_w_in, a_conv_w, b_mix_w, b_scale, ab_w_out, cd_w_in, c_q_norm_g, c_w_uq, c_kv_norm_g, c_w_ukv, d_ln_g, d_ln_b, d_w_s, d_b_s, cd_w_out, ffn_w_up, ffn_conv_w, ffn_w_down, final_norm_g):
    mod = _modulation(c, ada_w, ada_b).reshape(2, BATCH, N_MOD, D_MODEL)

    x = _l0_mixer(x, mod[0], norm1_g[0], ab_w_in[0].astype(BF16), a_conv_w[0],
                  b_mix_w[0].astype(BF16), b_scale[0], ab_w_out[0].astype(BF16))
    x = _conv_ffn(x, mod[0], norm2_g[0], *_ffn_weights(ffn_w_up[0], ffn_conv_w[0], ffn_w_down[0]))

    w_in = cd_w_in[0]
    c0 = C_Q_RANK + C_KV_RANK
    kpe_cols = jnp.pad(w_in[:, c0:c0 + C_ROPE], ((0, 0), (C_NOPE, HEAD_SLOT - C_NOPE - C_ROPE)))
    w_in = jnp.concatenate([w_in[:, :c0], kpe_cols, w_in[:, c0 + C_ROPE:]], axis=1).astype(BF16)
    w_uq = _pad_heads(c_w_uq[0], C_NOPE + C_ROPE).astype(BF16)
    w_ukv = c_w_ukv[0].reshape(C_KV_RANK, C_HEADS, C_NOPE + C_V)
    w_k = _pad_heads(w_ukv[:, :, :C_NOPE].reshape(C_KV_RANK, -1), C_NOPE).astype(BF16)
    w_v = w_ukv[:, :, C_NOPE:].reshape(C_KV_RANK, C_WIDTH).astype(BF16)
    bias_map = jnp.repeat(d_b_s[0].T, D_GROUP_DIM, axis=1)
    rope = _rope_tables(positions)
    q, k, v, yd = _l1_in(x, mod[1], norm1_g[1], w_in, c_q_norm_g[0], w_uq, c_kv_norm_g[0], w_k, w_v,
                         rope, d_ln_g[0], d_ln_b[0], d_w_s[0], bias_map)
    x = _attention(q, k, v, yd, x, mod[1], cd_w_out[0].astype(BF16))
    return _conv_ffn(x, mod[1], norm2_g[1], *_ffn_weights(ffn_w_up[1], ffn_conv_w[1], ffn_w_down[1]),
                     final_g=final_norm_g)
```

```python
import functools
import math

import jax
import jax.numpy as jnp
from jax import lax
from jax.experimental import pallas as pl
from jax.experimental.pallas import tpu as pltpu

F32 = jnp.float32
BF16 = jnp.bfloat16

D_MODEL = 1024
BATCH = 8
SEQ = 2048
EPS = 1e-6

A_WIDTH = 512
CONV_WIDTH = 3
POOL_WINDOWS = (2, 4, 8, 16)
B_GROUP_DIM = 128
B_WIDTH = 512

C_HEADS = 8
C_NOPE = 64
C_ROPE = 32
C_V = 64
C_Q_RANK = 256
C_KV_RANK = 128
C_WIDTH = 512
ROPE_THETA = 10000.0
D_GROUPS = 4
D_GROUP_DIM = 128
D_WIDTH = 512
D_CHUNK = 128

D_FF = 2816
N_MOD = 6

LANES = 128
SUBLANES = 8
HEAD_SLOT = LANES
SEQ_TILE = 512
FF_CHUNK = 256
N_FF_CHUNKS = D_FF // FF_CHUNK
POOL_HALO = 16
CONV_HALO = SUBLANES
VMEM_LIMIT = 56 * 1024 * 1024
LOG2E = math.log2(math.e)


def _dot(a, b):
    return jnp.dot(a, b, preferred_element_type=F32)


def _dot_nt(a, b):
    return lax.dot_general(a, b, (((1,), (1,)), ((), ())), preferred_element_type=F32)


def _rms_rows(x):
    return x * lax.rsqrt(jnp.mean(x * x, axis=-1, keepdims=True) + EPS)


def _silu(x):
    return x * (1.0 / (1.0 + jnp.exp(-x)))


def _params(n_grid_axes, flags=None):
    return pltpu.CompilerParams(
        dimension_semantics=("arbitrary",) * n_grid_axes, vmem_limit_bytes=VMEM_LIMIT, flags=flags)


def _const_spec(shape):
    zeros = (0,) * len(shape)
    return pl.BlockSpec(shape, lambda *_: zeros, pipeline_mode=pl.Buffered(1))


def _mod_kernel(c_ref, w_ref, b_ref, o_ref):
    ca = _silu(c_ref[...]).astype(BF16)
    o_ref[0] = _dot(ca, w_ref[0].astype(BF16)) + b_ref[0]


def _modulation(c, ada_w, ada_b):
    depth, d, n = ada_w.shape
    tn = 1536
    return pl.pallas_call(
        _mod_kernel,
        out_shape=jax.ShapeDtypeStruct((depth, BATCH, n), F32),
        grid=(depth, n // tn),
        in_specs=[
            pl.BlockSpec((BATCH, d), lambda l, j: (0, 0)),
            pl.BlockSpec((1, d, tn), lambda l, j: (l, 0, j)),
            pl.BlockSpec((1, 1, tn), lambda l, j: (l, 0, j)),
        ],
        out_specs=pl.BlockSpec((1, BATCH, tn), lambda l, j: (l, 0, j)),
        compiler_params=_params(2),
        name="ada_modulation",
    )(c, ada_w, ada_b.reshape(depth, 1, n))


def _rope_kernel(ang_ref, cos_ref, sin_ref):
    a = ang_ref[...]
    cos_ref[...] = jnp.cos(a)
    sin_ref[...] = jnp.sin(a)


def _rope_tables(positions):
    half = C_ROPE // 2
    inv_freq = ROPE_THETA ** (-jnp.arange(half, dtype=F32) / half)
    ang = positions.astype(F32)[..., None] * inv_freq
    flat = ang.reshape(-1, LANES)
    cos, sin = pl.pallas_call(
        _rope_kernel,
        out_shape=(jax.ShapeDtypeStruct(flat.shape, F32),) * 2,
        name="rope_tables",
    )(flat)
    cos = cos.reshape(BATCH, SEQ, half)
    sin = sin.reshape(BATCH, SEQ, half)
    pad = jnp.zeros((BATCH, SEQ, LANES - C_ROPE), F32)
    return jnp.concatenate([cos, sin, pad], axis=-1)


def _rope_lane_maps(cs):
    half = C_ROPE // 2
    lane = lax.broadcasted_iota(jnp.int32, cs.shape, 1)
    in_lo = (lane >= C_NOPE) & (lane < C_NOPE + half)
    in_hi = (lane >= C_NOPE + half) & (lane < C_NOPE + C_ROPE)
    at_lo = pltpu.roll(cs, C_NOPE, 1)
    cos_hi = pltpu.roll(cs, C_NOPE + half, 1)
    sin_lo = pltpu.roll(cs, C_NOPE - half, 1)
    mul = jnp.where(lane < C_NOPE, 1.0, jnp.where(in_lo, at_lo, jnp.where(in_hi, cos_hi, 0.0)))
    lo = jnp.where(in_lo, -sin_lo, 0.0)
    hi = jnp.where(in_hi, at_lo, 0.0)
    return mul, lo, hi


def _apply_rope(x, mul, lo, hi):
    return (x * mul + pltpu.roll(x, HEAD_SLOT - C_ROPE // 2, 1) * lo
            + pltpu.roll(x, C_ROPE // 2, 1) * hi)


def _l0_mixer_kernel(x_ref, mod_ref, ng_ref, win_ref, cw_ref, mixw_ref, bscale_ref, wout_ref,
                     o_ref, cbuf, pbuf):
    s = pl.program_id(1)
    ts = SEQ_TILE

    @pl.when(s == 0)
    def _():
        cbuf[...] = jnp.zeros((CONV_HALO, A_WIDTH), F32)
        pbuf[0:POOL_HALO, :] = jnp.zeros((POOL_HALO, B_WIDTH), F32)

    x = x_ref[0]
    m = mod_ref[0]
    h = (_rms_rows(x) * (ng_ref[...] * (1.0 + m[1:2])) + m[0:1]).astype(BF16)

    zb = _dot(h, win_ref[:, 0:A_WIDTH])
    zc = _dot(h, win_ref[:, A_WIDTH:2 * A_WIDTH])
    za = _dot(h, win_ref[:, 2 * A_WIDTH:3 * A_WIDTH])
    zp = _dot(h, win_ref[:, 3 * A_WIDTH:])

    ch = zc * za
    prev = cbuf[...]
    cbuf[...] = ch[ts - CONV_HALO:, :]
    cw = cw_ref[...]
    ya = zb * (cw[0:1] * _shift_rows(ch, prev, 2) + cw[1:2] * _shift_rows(ch, prev, 1) + cw[2:3] * ch)

    pbuf[POOL_HALO:POOL_HALO + ts, :] = zp
    t = s * ts + lax.broadcasted_iota(jnp.int32, (ts, 1), 0)
    yb = []
    for g, w in enumerate(POOL_WINDOWS):
        lanes = slice(g * B_GROUP_DIM, (g + 1) * B_GROUP_DIM)
        pg = zp[:, lanes]
        acc = pg
        for k in range(1, w):
            acc = acc + pbuf[POOL_HALO - k:POOL_HALO - k + ts, lanes]
        cnt = jnp.minimum(t + 1, w).astype(F32)
        pm = (acc / cnt - pg).astype(BF16)
        yb.append(_dot(pm, mixw_ref[g]))
    yb = jnp.concatenate(yb, axis=-1) * bscale_ref[...]
    pbuf[0:POOL_HALO, :] = zp[ts - POOL_HALO:, :]

    cat = jnp.concatenate([ya.astype(BF16), yb.astype(BF16)], axis=-1)
    o_ref[0] = x + m[2:3] * _dot(cat, wout_ref[...])


def _l0_mixer(x, mod, norm_g, w_in, conv_w, mix_w, b_scale, w_out):
    ts = SEQ_TILE
    tile = pl.BlockSpec((1, ts, D_MODEL), lambda b, s: (b, s, 0))
    return pl.pallas_call(
        _l0_mixer_kernel,
        out_shape=jax.ShapeDtypeStruct(x.shape, F32),
        grid=(BATCH, SEQ // ts),
        in_specs=[
            tile,
            pl.BlockSpec((1, N_MOD, D_MODEL), lambda b, s: (b, 0, 0)),
            _const_spec((1, D_MODEL)),
            _const_spec(w_in.shape),
            _const_spec(conv_w.shape),
            _const_spec(mix_w.shape),
            _const_spec((1, B_WIDTH)),
            _const_spec(w_out.shape),
        ],
        out_specs=tile,
        scratch_shapes=[
            pltpu.VMEM((CONV_HALO, A_WIDTH), F32),
            pltpu.VMEM((POOL_HALO + ts, B_WIDTH), F32),
        ],
        compiler_params=_params(2),
        name="l0_mixer",
    )(x, mod, norm_g.reshape(1, -1), w_in, conv_w, mix_w, b_scale.reshape(1, -1), w_out)


def _shift_rows(z, prev, k):
    ext = jnp.concatenate([prev, z], axis=0)
    return ext[CONV_HALO - k:CONV_HALO - k + z.shape[0], :]


def _ffn_kernel(*refs, final_norm):
    if final_norm:
        x_ref, mod_ref, ng_ref, wup_ref, cw_ref, wdn_ref, fg_ref, o_ref, zprev, acc_ref = refs
    else:
        x_ref, mod_ref, ng_ref, wup_ref, cw_ref, wdn_ref, o_ref, zprev, acc_ref = refs
    s = pl.program_id(1)
    ts = SEQ_TILE

    @pl.when(s == 0)
    def _():
        zprev[...] = jnp.zeros(zprev.shape, F32)

    x = x_ref[0]
    m = mod_ref[0]
    h = (_rms_rows(x) * (ng_ref[...] * (1.0 + m[4:5])) + m[3:4]).astype(BF16)

    def up_cols(c, half):
        return slice(half * D_FF + c * FF_CHUNK, half * D_FF + (c + 1) * FF_CHUNK)

    def up_proj(c):
        return [_dot(h, wup_ref[:, up_cols(c, half)]) for half in range(2)]

    def down_proj(c, a):
        contrib = _dot(a, wdn_ref[c * FF_CHUNK:(c + 1) * FF_CHUNK, :])
        if c == 0:
            acc_ref[...] = contrib
        else:
            acc_ref[...] += contrib

    z_next = up_proj(0)
    a_prev = None
    for c in range(N_FF_CHUNKS):
        z_cur = z_next
        if c + 1 < N_FF_CHUNKS:
            z_next = up_proj(c + 1)
        if a_prev is not None:
            down_proj(c - 1, a_prev)
        gu = []
        for half, z in enumerate(z_cur):
            cols = up_cols(c, half)
            prev = zprev[:, cols]
            zprev[:, cols] = z[ts - CONV_HALO:, :]
            cw = cw_ref[:, cols]
            gu.append(cw[0:1] * _shift_rows(z, prev, 2) + cw[1:2] * _shift_rows(z, prev, 1)
                      + cw[2:3] * z)
        a_prev = (_silu(gu[0]) * gu[1]).astype(BF16)
    down_proj(N_FF_CHUNKS - 1, a_prev)

    y = x + m[5:6] * acc_ref[...]
    if final_norm:
        y = _rms_rows(y) * fg_ref[...]
    o_ref[0] = y


def _conv_ffn(x, mod, norm_g, w_up, conv_w, w_down, final_g=None):
    ts = SEQ_TILE
    final_norm = final_g is not None
    tile = pl.BlockSpec((1, ts, D_MODEL), lambda b, s: (b, s, 0))
    in_specs = [
        tile,
        pl.BlockSpec((1, N_MOD, D_MODEL), lambda b, s: (b, 0, 0)),
        _const_spec((1, D_MODEL)),
        _const_spec(w_up.shape),
        _const_spec(conv_w.shape),
        _const_spec(w_down.shape),
    ]
    args = [x, mod, norm_g.reshape(1, -1), w_up, conv_w, w_down]
    if final_norm:
        in_specs.append(_const_spec((1, D_MODEL)))
        args.append(final_g.reshape(1, -1))
    return pl.pallas_call(
        functools.partial(_ffn_kernel, final_norm=final_norm),
        out_shape=jax.ShapeDtypeStruct(x.shape, F32),
        grid=(BATCH, SEQ // ts),
        in_specs=in_specs,
        out_specs=tile,
        scratch_shapes=[
            pltpu.VMEM((CONV_HALO, 2 * D_FF), F32),
            pltpu.VMEM((ts, D_MODEL), F32),
        ],
        compiler_params=_params(2),
        name="conv_ffn_final" if final_norm else "conv_ffn",
    )(*args)


def _l1_in_kernel(x_ref, mod_ref, ng_ref, win_ref, qg_ref, wuq_ref, kvg_ref, wk_ref, wv_ref,
                  rope_ref, lng_ref, lnb_ref, ws_ref, bias_ref,
                  q_ref, k_ref, v_ref, yd_ref):
    ts = SEQ_TILE
    x = x_ref[0]
    m = mod_ref[0]
    h = (_rms_rows(x) * (ng_ref[...] * (1.0 + m[1:2])) + m[0:1]).astype(BF16)

    mul, lo, hi = _rope_lane_maps(rope_ref[0])

    q_lat = _dot(h, win_ref[:, 0:C_Q_RANK])
    qn = (_rms_rows(q_lat) * qg_ref[...]).astype(BF16)
    q = _dot(qn, wuq_ref[...])
    for hd in range(C_HEADS):
        lanes = slice(hd * HEAD_SLOT, (hd + 1) * HEAD_SLOT)
        q_ref[0, :, lanes] = _apply_rope(q[:, lanes], mul, lo, hi).astype(BF16)

    c0 = C_Q_RANK
    kv_lat = _dot(h, win_ref[:, c0:c0 + C_KV_RANK])
    kvn = (_rms_rows(kv_lat) * kvg_ref[...]).astype(BF16)
    c0 += C_KV_RANK
    kpe = _apply_rope(_dot(h, win_ref[:, c0:c0 + HEAD_SLOT]), mul, lo, hi)
    kn = _dot(kvn, wk_ref[...])
    for hd in range(C_HEADS):
        lanes = slice(hd * HEAD_SLOT, (hd + 1) * HEAD_SLOT)
        k_ref[0, :, lanes] = (kn[:, lanes] + kpe).astype(BF16)
    v_ref[0, 0] = _dot_nt(wv_ref[...], kvn).astype(BF16)
    c0 += HEAD_SLOT

    uv = _dot(h, win_ref[:, c0:])
    uv = 0.5 * uv * (1.0 + jnp.tanh(math.sqrt(2.0 / math.pi) * (uv + 0.044715 * (uv * uv * uv))))
    u, v2 = uv[:, 0:D_WIDTH], uv[:, D_WIDTH:]
    mu = jnp.mean(v2, axis=-1, keepdims=True)
    vc = v2 - mu
    vn = vc * lax.rsqrt(jnp.mean(vc * vc, axis=-1, keepdims=True) + EPS)
    vn = (vn * lng_ref[...] + lnb_ref[...]).astype(BF16)
    row = lax.broadcasted_iota(jnp.int32, (D_CHUNK, D_CHUNK), 0)
    col = lax.broadcasted_iota(jnp.int32, (D_CHUNK, D_CHUNK), 1)
    bias = bias_ref[...]
    for g in range(D_GROUPS):
        lanes = slice(g * D_GROUP_DIM, (g + 1) * D_GROUP_DIM)
        wg = jnp.where(col <= row, ws_ref[g], 0.0).astype(BF16)
        for ck in range(ts // D_CHUNK):
            rows = slice(ck * D_CHUNK, (ck + 1) * D_CHUNK)
            mixed = _dot(wg, vn[rows, lanes]) + bias[:, lanes]
            yd_ref[0, rows, lanes] = (u[rows, lanes] * mixed).astype(BF16)


def _l1_in(x, mod, norm_g, w_in, q_g, w_uq, kv_g, w_k, w_v, rope, ln_g, ln_b, w_s, bias_map):
    ts = SEQ_TILE
    grid_tile = lambda width: pl.BlockSpec((1, ts, width), lambda b, s: (b, s, 0))
    qk_width = C_HEADS * HEAD_SLOT
    return pl.pallas_call(
        _l1_in_kernel,
        out_shape=(
            jax.ShapeDtypeStruct((BATCH, SEQ, qk_width), BF16),
            jax.ShapeDtypeStruct((BATCH, SEQ, qk_width), BF16),
            jax.ShapeDtypeStruct((BATCH, SEQ // ts, C_WIDTH, ts), BF16),
            jax.ShapeDtypeStruct((BATCH, SEQ, D_WIDTH), BF16),
        ),
        grid=(BATCH, SEQ // ts),
        in_specs=[
            grid_tile(D_MODEL),
            pl.BlockSpec((1, N_MOD, D_MODEL), lambda b, s: (b, 0, 0)),
            _const_spec((1, D_MODEL)),
            _const_spec(w_in.shape),
            _const_spec((1, C_Q_RANK)),
            _const_spec(w_uq.shape),
            _const_spec((1, C_KV_RANK)),
            _const_spec(w_k.shape),
            _const_spec(w_v.shape),
            grid_tile(LANES),
            _const_spec((1, D_WIDTH)),
            _const_spec((1, D_WIDTH)),
            _const_spec(w_s.shape),
            _const_spec(bias_map.shape),
        ],
        out_specs=(grid_tile(qk_width), grid_tile(qk_width),
                   pl.BlockSpec((1, 1, C_WIDTH, ts), lambda b, s: (b, s, 0, 0)), grid_tile(D_WIDTH)),
        compiler_params=_params(2),
        name="l1_in_proj",
    )(x, mod, norm_g.reshape(1, -1), w_in, q_g.reshape(1, -1), w_uq, kv_g.reshape(1, -1), w_k, w_v,
      rope, ln_g.reshape(1, -1), ln_b.reshape(1, -1), w_s, bias_map)


def _attn_kernel(q_ref, k_ref, v_ref, yd_ref, x_ref, mod_ref, wout_ref, o_ref, yc_ref):
    i = pl.program_id(1)
    tq = SEQ_TILE
    tk = SEQ_TILE
    scale2 = (C_NOPE + C_ROPE) ** -0.5 * LOG2E
    kpos = lax.broadcasted_iota(jnp.int32, (tk, tq), 0)
    qpos = lax.broadcasted_iota(jnp.int32, (tk, tq), 1)

    def head_block(q, hd, j, carry, masked):
        m_prev, l_prev, acc = carry
        k = k_ref[0, pl.ds(pl.multiple_of(j * tk, tk), tk), hd * HEAD_SLOT:(hd + 1) * HEAD_SLOT]
        vt = v_ref[0, j, hd * C_V:(hd + 1) * C_V, :]
        st = _dot_nt(k, q)
        if masked:
            st = jnp.where(kpos <= qpos, st, -jnp.inf)
        m_new = jnp.maximum(m_prev, jnp.max(st, axis=0, keepdims=True))
        alpha = jnp.exp2((m_prev - m_new) * scale2)
        p = jnp.exp2((st - m_new) * scale2)
        l_new = alpha * l_prev + jnp.sum(p, axis=0, keepdims=True)
        acc = alpha * acc + _dot(vt, p.astype(BF16))
        return m_new, l_new, acc

    for pair in range(C_HEADS // 2):
        heads = (2 * pair, 2 * pair + 1)
        qs = [q_ref[0, :, hd * HEAD_SLOT:(hd + 1) * HEAD_SLOT] for hd in heads]

        def pair_block(j, carry, masked):
            return tuple(head_block(q, hd, j, c, masked) for q, hd, c in zip(qs, heads, carry))

        init = (jnp.full((1, tq), -jnp.inf, F32), jnp.zeros((1, tq), F32), jnp.zeros((C_V, tq), F32))
        carry = lax.fori_loop(0, i, lambda j, c: pair_block(j, c, False), (init, init))
        fin = pair_block(i, carry, True)
        out_t = jnp.concatenate([acc / l_fin for _, l_fin, acc in fin], axis=0)
        yc_ref[:, pair * LANES:(pair + 1) * LANES] = out_t.T.astype(BF16)

    cat = jnp.concatenate([yc_ref[...], yd_ref[0]], axis=-1)
    m = mod_ref[0]
    o_ref[0] = x_ref[0] + m[2:3] * _dot(cat, wout_ref[...])


def _attention(q, k, v, yd, x, mod, w_out):
    tq = SEQ_TILE
    qk_width = C_HEADS * HEAD_SLOT
    q_tile = lambda width: pl.BlockSpec((1, tq, width), lambda b, s: (b, s, 0))
    return pl.pallas_call(
        _attn_kernel,
        out_shape=jax.ShapeDtypeStruct(x.shape, F32),
        grid=(BATCH, SEQ // tq),
        in_specs=[
            q_tile(qk_width),
            pl.BlockSpec((1, SEQ, qk_width), lambda b, s: (b, 0, 0)),
            pl.BlockSpec((1, SEQ // tq, C_WIDTH, tq), lambda b, s: (b, 0, 0, 0)),
            q_tile(D_WIDTH),
            q_tile(D_MODEL),
            pl.BlockSpec((1, N_MOD, D_MODEL), lambda b, s: (b, 0, 0)),
            _const_spec(w_out.shape),
        ],
        out_specs=q_tile(D_MODEL),
        scratch_shapes=[pltpu.VMEM((tq, C_WIDTH), BF16)],
        compiler_params=_params(2),
        name="l1_attention",
    )(q, k, v, yd, x, mod, w_out)


def _ffn_weights(w_up, conv_w, w_down):
    return w_up.astype(BF16), conv_w, w_down.astype(BF16)


def _pad_heads(w, used):
    kdim = w.shape[0]
    w = w.reshape(kdim, C_HEADS, used)
    w = jnp.pad(w, ((0, 0), (0, 0), (0, HEAD_SLOT - used)))
    return w.reshape(kdim, C_HEADS * HEAD_SLOT)


def kernel(x, c, positions, ada_w, ada_b, norm1_g, norm2_g, ab_w_in, a_conv_w, b_mix_w, b_scale, ab_w_out, cd_w_in, c_q_norm_g, c_w_uq, c_kv_norm_g, c_w_ukv, d_ln_g, d_ln_b, d_w_s, d_b_s, cd_w_out, ffn_w_up, ffn_conv_w, ffn_w_down, final_norm_g):
    mod = _modulation(c, ada_w, ada_b).reshape(2, BATCH, N_MOD, D_MODEL)

    x = _l0_mixer(x, mod[0], norm1_g[0], ab_w_in[0].astype(BF16), a_conv_w[0],
                  b_mix_w[0].astype(BF16), b_scale[0], ab_w_out[0].astype(BF16))
    x = _conv_ffn(x, mod[0], norm2_g[0], *_ffn_weights(ffn_w_up[0], ffn_conv_w[0], ffn_w_down[0]))

    w_in = cd_w_in[0]
    c0 = C_Q_RANK + C_KV_RANK
    kpe_cols = jnp.pad(w_in[:, c0:c0 + C_ROPE], ((0, 0), (C_NOPE, HEAD_SLOT - C_NOPE - C_ROPE)))
    w_in = jnp.concatenate([w_in[:, :c0], kpe_cols, w_in[:, c0 + C_ROPE:]], axis=1).astype(BF16)
    w_uq = _pad_heads(c_w_uq[0], C_NOPE + C_ROPE).astype(BF16)
    w_ukv = c_w_ukv[0].reshape(C_KV_RANK, C_HEADS, C_NOPE + C_V)
    w_k = _pad_heads(w_ukv[:, :, :C_NOPE].reshape(C_KV_RANK, -1), C_NOPE).astype(BF16)
    w_v = w_ukv[:, :, C_NOPE:].reshape(C_KV_RANK, C_WIDTH).T.astype(BF16)
    bias_map = jnp.repeat(d_b_s[0].T, D_GROUP_DIM, axis=1)
    rope = _rope_tables(positions)
    q, k, v, yd = _l1_in(x, mod[1], norm1_g[1], w_in, c_q_norm_g[0], w_uq, c_kv_norm_g[0], w_k, w_v,
                         rope, d_ln_g[0], d_ln_b[0], d_w_s[0], bias_map)
    x = _attention(q, k, v, yd, x, mod[1], cd_w_out[0].astype(BF16))
    return _conv_ffn(x, mod[1], norm2_g[1], *_ffn_weights(ffn_w_up[1], ffn_conv_w[1], ffn_w_down[1]),
                     final_g=final_norm_g)
```
